```python
import jax, jax.numpy as jnp
from jax import lax
import numpy as np

D_MODEL = 1024
BATCH = 8
SEQ = 8192
DEPTH = 1
DEC_BATCH = 4
DEC_SEQ = 4096
PAST_LEN = 128

GRID_W = 64
CHUNK = 128
D_MIX = D_MODEL
D_A = D_MIX // 2
N_GROUPS_A = 4
GA = D_A // N_GROUPS_A
D_B = D_MIX - D_A
N_HEADS_B = 8
HEAD_DIM_B = D_B // N_HEADS_B
WIN_ROWS = 8
WIN_COLS = 16
REL_R = 2 * WIN_ROWS - 1
REL_C = 2 * WIN_COLS - 1
QBLOCK = 128
D_IN = 2 * D_A + 3 * D_B
N_EXPERTS = 32
TOP_K = 4
D_FF = D_MODEL
SWIGLU_LIMIT = 7.0
SWIGLU_ALPHA = 1.702
EPS = 1e-6

kernel_name = 'hymba_gmlp_natten_moe_encoder'


def rmsnorm(x, g):
    xf = x.astype(jnp.float32)
    y = xf * lax.rsqrt(jnp.mean(xf * xf, axis=-1, keepdims=True) + EPS)
    return (y * g.astype(jnp.float32)).astype(x.dtype)


def layernorm(x, g):
    xf = x.astype(jnp.float32)
    mu = jnp.mean(xf, axis=-1, keepdims=True)
    xc = xf - mu
    y = xc * lax.rsqrt(jnp.mean(xc * xc, axis=-1, keepdims=True) + EPS)
    return (y * g.astype(jnp.float32)).astype(x.dtype)


def modulate(h, shift, scale):
    return h * (1 + scale[:, None, :]) + shift[:, None, :]


def spatial_gating(u, v, w_s, b_s, g_sgu):
    b, n, _ = u.shape
    u = jax.nn.gelu(u, approximate=False)
    v = jax.nn.gelu(v, approximate=False)
    v = v.reshape(b, n // CHUNK, CHUNK, N_GROUPS_A, GA)
    v = layernorm(v, g_sgu.reshape(N_GROUPS_A, GA))
    v = jnp.einsum('gpq,bnqgc->bnpgc', w_s, v) + b_s.T[None, None, :, :, None]
    return u * v.reshape(b, n, D_A)


def neighbourhood_tables(n_tok):
    rows = n_tok // GRID_W
    kr = min(WIN_ROWS, rows)
    kc = min(WIN_COLS, GRID_W)
    t = jnp.arange(n_tok, dtype=jnp.int32)
    r = t // GRID_W
    c = t % GRID_W
    rs = jnp.clip(r - kr // 2, 0, rows - kr)
    cs = jnp.clip(c - kc // 2, 0, GRID_W - kc)
    key_r = rs[:, None, None] + jnp.arange(kr, dtype=jnp.int32)[None, :, None]
    key_c = cs[:, None, None] + jnp.arange(kc, dtype=jnp.int32)[None, None, :]
    idx = (key_r * GRID_W + key_c).reshape(n_tok, kr * kc)
    rel = ((key_r - r[:, None, None] + (WIN_ROWS - 1)) * REL_C
           + (key_c - c[:, None, None] + (WIN_COLS - 1))).reshape(n_tok, kr * kc)
    return idx, rel


def neighbourhood_attention(q, k, v, rpb):
    b, n, h, dh = q.shape
    idx, rel = neighbourhood_tables(n)
    n_blk = n // QBLOCK
    kk = idx.shape[-1]
    bias = rpb.reshape(h, REL_R * REL_C)[:, rel]
    bias_blk = bias.reshape(h, n_blk, QBLOCK, kk).transpose(1, 0, 2, 3)
    q_blk = q.reshape(b, n_blk, QBLOCK, h, dh).transpose(1, 0, 2, 3, 4)
    idx_blk = idx.reshape(n_blk, QBLOCK, kk)
    scale = HEAD_DIM_B ** -0.5

    def one_block(args):
        qb, ib, bb = args
        kb = k[:, ib]
        vb = v[:, ib]
        s = jnp.einsum('bqhd,bqkhd->bhqk', qb, kb).astype(jnp.float32) * scale
        s = s + bb[None].astype(jnp.float32)
        p = jax.nn.softmax(s, axis=-1).astype(vb.dtype)
        return jnp.einsum('bhqk,bqkhd->bqhd', p, vb)

    o = lax.map(one_block, (q_blk, idx_blk, bias_blk))
    return o.transpose(1, 0, 2, 3, 4).reshape(b, n, h * dh)


def moe_ffn(h, w_router, b_router, w_gu, b_gu, w_dn, b_dn):
    b, n, d = h.shape
    t = h.reshape(b * n, d)
    logits = (t @ w_router + b_router).astype(jnp.float32)
    top_val, top_idx = lax.top_k(logits, TOP_K)
    top_w = jax.nn.softmax(top_val, axis=-1)
    combine = jnp.sum(jax.nn.one_hot(top_idx, N_EXPERTS, dtype=jnp.float32) * top_w[..., None], axis=1)
    out = jnp.zeros((b * n, d), jnp.float32)
    for e in range(N_EXPERTS):
        gu = t @ w_gu[e] + b_gu[e]
        gate = jnp.minimum(gu[:, :D_FF], SWIGLU_LIMIT)
        up = jnp.clip(gu[:, D_FF:], -SWIGLU_LIMIT, SWIGLU_LIMIT)
        act = gate * jax.nn.sigmoid(SWIGLU_ALPHA * gate) * (up + 1)
        y = act @ w_dn[e] + b_dn[e]
        out = out + combine[:, e:e + 1] * y.astype(jnp.float32)
    return out.astype(h.dtype).reshape(b, n, d)


def trunk(x, c, w_ada, b_ada, g_mix, w_in, w_s, b_s, g_sgu, rpb, g_out_a, g_out_b, w_out,
          g_ffn, w_router, b_router, w_gu, b_gu, w_dn, b_dn, g_final):
    b, n, _ = x.shape
    for l in range(DEPTH):
        mod = jax.nn.silu(c) @ w_ada[l] + b_ada[l]
        sh_m, sc_m, gt_m, sh_f, sc_f, gt_f = jnp.split(mod, 6, axis=-1)
        hm = modulate(rmsnorm(x, g_mix[l]), sh_m, sc_m)
        proj = hm @ w_in[l]
        u_a, v_a, q_b, k_b, v_b = jnp.split(proj, [D_A, 2 * D_A, 2 * D_A + D_B, 2 * D_A + 2 * D_B], axis=-1)
        o_a = spatial_gating(u_a, v_a, w_s[l], b_s[l], g_sgu[l])
        hs = (b, n, N_HEADS_B, HEAD_DIM_B)
        o_b = neighbourhood_attention(q_b.reshape(hs), k_b.reshape(hs), v_b.reshape(hs), rpb[l])
        mixed = jnp.concatenate([rmsnorm(o_a, g_out_a[l]), rmsnorm(o_b, g_out_b[l])], axis=-1) @ w_out[l]
        x = x + gt_m[:, None, :] * mixed
        hf = modulate(rmsnorm(x, g_ffn[l]), sh_f, sc_f)
        x = x + gt_f[:, None, :] * moe_ffn(hf, w_router[l], b_router[l], w_gu[l], b_gu[l], w_dn[l], b_dn[l])
    return rmsnorm(x, g_final)


def setup_inputs(seed: int = 0) -> dict:
    key = jax.random.key(seed)
    ks = jax.random.split(key, 24)
    f32 = jnp.float32
    nrm = lambda k, shape, s: jax.random.normal(k, shape, f32) * s
    d = D_MODEL
    return {
        'x_prompt': nrm(ks[0], (BATCH, SEQ, d), 1.0),
        'x_sample': nrm(ks[1], (DEC_BATCH, DEC_SEQ, d), 1.0),
        'c_prompt': nrm(ks[2], (BATCH, d), 1.0),
        'c_sample': nrm(ks[3], (DEC_BATCH, d), 1.0),
        'w_ada': nrm(ks[4], (DEPTH, d, 6 * d), 0.25 * d ** -0.5),
        'b_ada': nrm(ks[5], (DEPTH, 6 * d), 0.02),
        'g_mix': 1.0 + nrm(ks[6], (DEPTH, d), 0.05),
        'w_in': nrm(ks[7], (DEPTH, d, D_IN), d ** -0.5),
        'w_s': nrm(ks[8], (DEPTH, N_GROUPS_A, CHUNK, CHUNK), CHUNK ** -0.5),
        'b_s': 1.0 + nrm(ks[9], (DEPTH, N_GROUPS_A, CHUNK), 0.1),
        'g_sgu': 1.0 + nrm(ks[10], (DEPTH, D_A), 0.05),
        'rpb': nrm(ks[11], (DEPTH, N_HEADS_B, REL_R, REL_C), 0.1),
        'g_out_a': 1.0 + nrm(ks[12], (DEPTH, D_A), 0.05),
        'g_out_b': 1.0 + nrm(ks[13], (DEPTH, D_B), 0.05),
        'w_out': nrm(ks[14], (DEPTH, D_MIX, d), D_MIX ** -0.5),
        'g_ffn': 1.0 + nrm(ks[15], (DEPTH, d), 0.05),
        'w_router': nrm(ks[16], (DEPTH, d, N_EXPERTS), d ** -0.5),
        'b_router': nrm(ks[17], (DEPTH, N_EXPERTS), 0.01),
        'w_gu': nrm(ks[18], (DEPTH, N_EXPERTS, d, 2 * D_FF), d ** -0.5),
        'b_gu': nrm(ks[19], (DEPTH, N_EXPERTS, 2 * D_FF), 0.01),
        'w_dn': nrm(ks[20], (DEPTH, N_EXPERTS, D_FF, d), D_FF ** -0.5),
        'b_dn': nrm(ks[21], (DEPTH, N_EXPERTS, d), 0.01),
        'g_final': 1.0 + nrm(ks[22], (d,), 0.05),
    }


def reference(x_prompt, x_sample, c_prompt, c_sample, w_ada, b_ada, g_mix, w_in, w_s, b_s, g_sgu, rpb,
              g_out_a, g_out_b, w_out, g_ffn, w_router, b_router, w_gu, b_gu, w_dn, b_dn, g_final):
    y_prompt = trunk(x_prompt, c_prompt, w_ada, b_ada, g_mix, w_in, w_s, b_s, g_sgu, rpb, g_out_a, g_out_b,
                     w_out, g_ffn, w_router, b_router, w_gu, b_gu, w_dn, b_dn, g_final)
    y_sample = trunk(x_sample, c_sample, w_ada, b_ada, g_mix, w_in, w_s, b_s, g_sgu, rpb, g_out_a, g_out_b,
                     w_out, g_ffn, w_router, b_router, w_gu, b_gu, w_dn, b_dn, g_final)
    return (y_prompt, y_sample)
```

```python
import functools

import jax
import jax.numpy as jnp
from jax import lax
from jax.experimental import pallas as pl
from jax.experimental.pallas import tpu as pltpu

F32 = jnp.float32
BF16 = jnp.bfloat16

D_MODEL = 1024
GRID_W = 64
CHUNK = 128
D_A = 512
N_GROUPS_A = 4
GA = D_A // N_GROUPS_A
D_B = 512
N_HEADS = 8
HEAD_DIM = D_B // N_HEADS
WIN_ROWS = 8
WIN_COLS = 16
REL_R = 2 * WIN_ROWS - 1
REL_C = 2 * WIN_COLS - 1
N_EXPERTS = 32
TOP_K = 4
D_FF = D_MODEL
SWIGLU_LIMIT = 7.0
SWIGLU_ALPHA = 1.702
EPS = 1e-6

LANES = 128
NEG = -1e30
Q_ROWS = 4
Q_TOK = Q_ROWS * GRID_W
K_BLOCKS = 3
TM_MIX = 512
TM_EXP = 512
TM_ROW = 256
VMEM_LIMIT = 56 * 1024 * 1024


def _rms(x):
    return x * lax.rsqrt(jnp.mean(x * x, axis=-1, keepdims=True) + EPS)


def _gelu(x):
    return 0.5 * x * (1.0 + lax.erf(x * 0.7071067811865476))


def _ada_kernel(c_ref, w_ref, b_ref, o_ref):
    c = c_ref[...]
    s = c / (1.0 + jnp.exp(-c))
    o_ref[...] = jnp.dot(s, w_ref[...], preferred_element_type=F32,
                         precision=lax.Precision.HIGHEST) + b_ref[...]


def _ada(c, w, b):
    r, d = c.shape
    n = w.shape[1]
    tn = 1536
    return pl.pallas_call(
        _ada_kernel,
        grid=(n // tn,),
        in_specs=[pl.BlockSpec((r, d), lambda j: (0, 0)),
                  pl.BlockSpec((d, tn), lambda j: (0, j)),
                  pl.BlockSpec((1, tn), lambda j: (0, j))],
        out_specs=pl.BlockSpec((r, tn), lambda j: (0, j)),
        out_shape=jax.ShapeDtypeStruct((r, n), F32),
        name="ada",
    )(c, w, b)


def _mix_in_kernel(x_ref, mod_ref, gmix_ref, win_ref, ws_ref, bs_ref, gsgu_ref, gouta_ref,
                   an_ref, q_ref, k_ref, v_ref):
    x = x_ref[...]
    m = mod_ref[0]
    h = _rms(x) * gmix_ref[...]
    h = h * (1.0 + m[1:2, :]) + m[0:1, :]
    hb = h.astype(BF16)

    u = _gelu(jnp.dot(hb, win_ref[:, 0:D_A], preferred_element_type=F32))
    v = _gelu(jnp.dot(hb, win_ref[:, D_A:2 * D_A], preferred_element_type=F32))
    for j in range(x.shape[0] // CHUNK):
        rows = slice(j * CHUNK, (j + 1) * CHUNK)
        blocks = []
        ssq = None
        for g in range(N_GROUPS_A):
            cols = slice(g * GA, (g + 1) * GA)
            vg = v[rows, cols]
            xc = vg - jnp.mean(vg, axis=-1, keepdims=True)
            vn = xc * lax.rsqrt(jnp.mean(xc * xc, axis=-1, keepdims=True) + EPS) * gsgu_ref[:, cols]
            sp = jnp.dot(ws_ref[g], vn.astype(BF16), preferred_element_type=F32) + bs_ref[g]
            o = u[rows, cols] * sp
            blocks.append(o)
            s = jnp.sum(o * o, axis=-1, keepdims=True)
            ssq = s if ssq is None else ssq + s
        inv = lax.rsqrt(ssq * (1.0 / D_A) + EPS)
        for g in range(N_GROUPS_A):
            cols = slice(g * GA, (g + 1) * GA)
            an_ref[rows, cols] = (blocks[g] * inv * gouta_ref[:, cols]).astype(BF16)

    o0 = 2 * D_A
    q = jnp.dot(hb, win_ref[:, o0:o0 + D_B], preferred_element_type=F32)
    q_ref[...] = (q * (HEAD_DIM ** -0.5)).astype(BF16)
    k_ref[...] = jnp.dot(hb, win_ref[:, o0 + D_B:o0 + 2 * D_B], preferred_element_type=F32).astype(BF16)
    v_ref[...] = jnp.dot(hb, win_ref[:, o0 + 2 * D_B:o0 + 3 * D_B], preferred_element_type=F32).astype(BF16)


def _mix_in(x2, mod6, gmix, win, ws, bs, gsgu, gouta, n_seq):
    t, d = x2.shape
    tm = TM_MIX
    per_seq = n_seq // tm
    row = lambda i: (i, 0)
    const2 = lambda i: (0, 0)
    const3 = lambda i: (0, 0, 0)
    out = jax.ShapeDtypeStruct((t, D_A), BF16)
    return pl.pallas_call(
        _mix_in_kernel,
        grid=(t // tm,),
        in_specs=[pl.BlockSpec((tm, d), row),
                  pl.BlockSpec((1, 6, d), lambda i: (i // per_seq, 0, 0)),
                  pl.BlockSpec((1, d), const2),
                  pl.BlockSpec(win.shape, const2),
                  pl.BlockSpec(ws.shape, const3),
                  pl.BlockSpec(bs.shape, const3),
                  pl.BlockSpec((1, D_A), const2),
                  pl.BlockSpec((1, D_A), const2)],
        out_specs=[pl.BlockSpec((tm, D_A), row)] * 4,
        out_shape=[out] * 4,
        compiler_params=pltpu.CompilerParams(vmem_limit_bytes=VMEM_LIMIT),
        name="mix_in",
    )(x2, mod6, gmix, win, ws, bs, gsgu, gouta)


def _attn_bias_tables(rpb):
    rows = 32
    tables = []
    for r0, kb in ((0, 0), (4, 0), (rows - Q_ROWS, rows - K_BLOCKS * Q_ROWS)):
        r = r0 + jnp.arange(Q_ROWS)[:, None, None, None]
        c = jnp.arange(GRID_W)[None, :, None, None]
        kr = kb + jnp.arange(K_BLOCKS * Q_ROWS)[None, None, :, None]
        kc = jnp.arange(GRID_W)[None, None, None, :]
        rs = jnp.clip(r - WIN_ROWS // 2, 0, rows - WIN_ROWS)
        cs = jnp.clip(c - WIN_COLS // 2, 0, GRID_W - WIN_COLS)
        valid = (kr >= rs) & (kr < rs + WIN_ROWS) & (kc >= cs) & (kc < cs + WIN_COLS)
        rel_r = jnp.clip(kr - r + (WIN_ROWS - 1), 0, REL_R - 1)
        rel_c = jnp.clip(kc - c + (WIN_COLS - 1), 0, REL_C - 1)
        rel_r, rel_c, valid = jnp.broadcast_arrays(rel_r, rel_c, valid)
        vals = rpb[:, rel_r, rel_c]
        tab = jnp.where(valid[None], vals, NEG)
        tables.append(tab.reshape(N_HEADS, Q_TOK, K_BLOCKS * Q_TOK))
    return jnp.stack(tables).astype(F32)


def _attn_kernel(q_ref, k0_ref, k1_ref, k2_ref, v0_ref, v1_ref, v2_ref, bias_ref, g_ref, o_ref, obuf):
    k_refs = (k0_ref, k1_ref, k2_ref)
    v_refs = (v0_ref, v1_ref, v2_ref)
    nt = (((1,), (1,)), ((), ()))
    for h in range(N_HEADS):
        cols = slice(h * HEAD_DIM, (h + 1) * HEAD_DIM)
        qh = q_ref[:, cols]
        s = [lax.dot_general(qh, k_refs[j][:, cols], nt, preferred_element_type=F32)
             + bias_ref[0, h, :, j * Q_TOK:(j + 1) * Q_TOK] for j in range(K_BLOCKS)]
        mx = jnp.maximum(jnp.maximum(jnp.max(s[0], axis=-1, keepdims=True),
                                     jnp.max(s[1], axis=-1, keepdims=True)),
                         jnp.max(s[2], axis=-1, keepdims=True))
        p = [jnp.exp(sj - mx) for sj in s]
        l = (jnp.sum(p[0], axis=-1, keepdims=True) + jnp.sum(p[1], axis=-1, keepdims=True)
             + jnp.sum(p[2], axis=-1, keepdims=True))
        o = (jnp.dot(p[0].astype(BF16), v_refs[0][:, cols], preferred_element_type=F32)
             + jnp.dot(p[1].astype(BF16), v_refs[1][:, cols], preferred_element_type=F32)
             + jnp.dot(p[2].astype(BF16), v_refs[2][:, cols], preferred_element_type=F32))
        obuf[:, cols] = o / l
    o = obuf[...]
    o_ref[...] = (_rms(o) * g_ref[...]).astype(BF16)


def _attn(q, k, v, bias, goutb, batch, n_seq):
    nq = n_seq // Q_TOK
    assert nq >= K_BLOCKS
    blk = (Q_TOK, D_B)

    def kv_map(j):
        return lambda b, i: (b * nq + jnp.clip(i - 1, 0, nq - K_BLOCKS) + j, 0)

    def bias_map(b, i):
        return (jnp.where(i == 0, 0, jnp.where(i == nq - 1, 2, 1)), 0, 0, 0)

    q_map = lambda b, i: (b * nq + i, 0)
    kv_specs = [pl.BlockSpec(blk, kv_map(j)) for j in range(K_BLOCKS)]
    return pl.pallas_call(
        _attn_kernel,
        grid=(batch, nq),
        in_specs=[pl.BlockSpec(blk, q_map)] + kv_specs + kv_specs
                 + [pl.BlockSpec((1,) + bias.shape[1:], bias_map),
                    pl.BlockSpec((1, D_B), lambda b, i: (0, 0))],
        out_specs=pl.BlockSpec(blk, q_map),
        out_shape=jax.ShapeDtypeStruct(q.shape, BF16),
        scratch_shapes=[pltpu.VMEM(blk, F32)],
        compiler_params=pltpu.CompilerParams(vmem_limit_bytes=VMEM_LIMIT),
        name="attn",
    )(q, k, k, k, v, v, v, bias, goutb)


def _mix_out_kernel(an_ref, bn_ref, x_ref, mod_ref, woa_ref, wob_ref, gffn_ref, wrh_ref, wrl_ref, br_ref,
                    cnt0_ref, x1_ref, hf_ref, route_ref, cnt_ref, run_ref):
    @pl.when(pl.program_id(0) == 0)
    def _():
        run_ref[...] = cnt0_ref[...]

    m = mod_ref[0]
    mixed = (jnp.dot(an_ref[...], woa_ref[...], preferred_element_type=F32)
             + jnp.dot(bn_ref[...], wob_ref[...], preferred_element_type=F32))
    x1 = x_ref[...] + m[2:3, :] * mixed
    x1_ref[...] = x1
    hf = _rms(x1) * gffn_ref[...]
    hf = hf * (1.0 + m[4:5, :]) + m[3:4, :]
    hf_ref[...] = hf

    h_hi = hf.astype(BF16)
    h_lo = (hf - h_hi.astype(F32)).astype(BF16)
    logits = (jnp.dot(h_hi, wrh_ref[...], preferred_element_type=F32)
              + jnp.dot(h_lo, wrh_ref[...], preferred_element_type=F32)
              + jnp.dot(h_hi, wrl_ref[...], preferred_element_type=F32)) + br_ref[...]

    tm = logits.shape[0]
    lane = lax.broadcasted_iota(jnp.int32, (tm, LANES), 1).astype(F32)
    vals, idxs = [], []
    l = logits
    for _ in range(TOP_K):
        mx = jnp.max(l, axis=-1, keepdims=True)
        ik = jnp.min(jnp.where(l == mx, lane, float(LANES)), axis=-1, keepdims=True)
        vals.append(mx)
        idxs.append(ik)
        l = jnp.where(lane == ik, -3e38, l)
    es = [jnp.exp(vk - vals[0]) for vk in vals]
    den = es[0] + es[1] + es[2] + es[3]
    ws = [e / den for e in es]

    hot = jnp.zeros((tm, LANES), F32)
    for ik in idxs:
        hot = hot + jnp.where(lane == ik, 1.0, 0.0)
    ri = lax.broadcasted_iota(jnp.int32, (tm, tm), 0)
    ci = lax.broadcasted_iota(jnp.int32, (tm, tm), 1)
    tri = jnp.where(ri > ci, 1.0, 0.0).astype(BF16)
    before = jnp.dot(tri, hot.astype(BF16), preferred_element_type=F32) + run_ref[...]
    ranks = [jnp.sum(jnp.where(lane == ik, before, 0.0), axis=-1, keepdims=True) for ik in idxs]
    run_ref[...] = run_ref[...] + jnp.sum(hot, axis=0, keepdims=True)
    cnt_ref[...] = run_ref[...]

    route = jnp.zeros((tm, LANES), F32)
    for k in range(TOP_K):
        route = jnp.where(lane == float(k), idxs[k], route)
        route = jnp.where(lane == float(TOP_K + k), ws[k], route)
        route = jnp.where(lane == float(2 * TOP_K + k), ranks[k], route)
    route_ref[...] = route


def _mix_out(an, bn, x2, mod6, woa, wob, gffn, wrh, wrl, br, cnt0, n_seq):
    t, d = x2.shape
    tm = TM_MIX
    per_seq = n_seq // tm
    row = lambda i: (i, 0)
    const2 = lambda i: (0, 0)
    return pl.pallas_call(
        _mix_out_kernel,
        grid=(t // tm,),
        in_specs=[pl.BlockSpec((tm, D_A), row),
                  pl.BlockSpec((tm, D_B), row),
                  pl.BlockSpec((tm, d), row),
                  pl.BlockSpec((1, 6, d), lambda i: (i // per_seq, 0, 0)),
                  pl.BlockSpec(woa.shape, const2),
                  pl.BlockSpec(wob.shape, const2),
                  pl.BlockSpec((1, d), const2),
                  pl.BlockSpec(wrh.shape, const2),
                  pl.BlockSpec(wrl.shape, const2),
                  pl.BlockSpec((1, LANES), const2),
                  pl.BlockSpec((1, LANES), const2)],
        out_specs=[pl.BlockSpec((tm, d), row),
                   pl.BlockSpec((tm, d), row),
                   pl.BlockSpec((tm, LANES), row),
                   pl.BlockSpec((1, LANES), const2)],
        out_shape=[jax.ShapeDtypeStruct((t, d), F32),
                   jax.ShapeDtypeStruct((t, d), F32),
                   jax.ShapeDtypeStruct((t, LANES), F32),
                   jax.ShapeDtypeStruct((1, LANES), F32)],
        scratch_shapes=[pltpu.VMEM((1, LANES), F32)],
        compiler_params=pltpu.CompilerParams(dimension_semantics=("arbitrary",),
                                             vmem_limit_bytes=VMEM_LIMIT),
        name="mix_out",
    )(an, bn, x2, mod6, woa, wob, gffn, wrh, wrl, br, cnt0)


def _dispatch_kernel(pos_ref, hf_ref, *rest):
    xs_ref, sem = rest[-2], rest[-1]
    tm = hf_ref.shape[0]

    def body(t, carry):
        for k in range(TOP_K):
            p = pos_ref[0, 0, TOP_K * t + k]
            pltpu.make_async_copy(hf_ref.at[pl.ds(t, 1)], xs_ref.at[pl.ds(p, 1)], sem).start()
        return carry

    lax.fori_loop(0, tm, body, 0)
    for _ in range(TOP_K):
        pltpu.make_async_copy(hf_ref, xs_ref.at[pl.ds(0, tm)], sem).wait()


def _dispatch(pos3, hf, xs_prev, n_rows):
    t, d = hf.shape
    tm = TM_ROW
    in_specs = [pl.BlockSpec((1, 1, TOP_K * tm), lambda i: (i, 0, 0), memory_space=pltpu.SMEM),
                pl.BlockSpec((tm, d), lambda i: (i, 0))]
    args = [pos3, hf]
    aliases = {}
    if xs_prev is not None:
        in_specs.append(pl.BlockSpec(memory_space=pl.ANY))
        args.append(xs_prev)
        aliases = {2: 0}
    return pl.pallas_call(
        _dispatch_kernel,
        grid=(t // tm,),
        in_specs=in_specs,
        out_specs=pl.BlockSpec(memory_space=pl.ANY),
        out_shape=jax.ShapeDtypeStruct((n_rows, d), F32),
        scratch_shapes=[pltpu.SemaphoreType.DMA],
        input_output_aliases=aliases,
        compiler_params=pltpu.CompilerParams(dimension_semantics=("arbitrary",),
                                             has_side_effects=True),
        name="dispatch",
    )(*args)


def _experts_kernel(te_ref, nu_ref, nv_ref, xs_ref, wgu_ref, bgu_ref, wdn_ref, bdn_ref, y_ref):
    i = pl.program_id(0)

    @pl.when(i < nu_ref[0])
    def _():
        rid = lax.broadcasted_iota(jnp.int32, xs_ref.shape, 0)
        xb = jnp.where(rid < nv_ref[i], xs_ref[...], 0.0).astype(BF16)
        gu = jnp.dot(xb, wgu_ref[0], preferred_element_type=F32) + bgu_ref[0]
        gate = jnp.minimum(gu[:, :D_FF], SWIGLU_LIMIT)
        up = jnp.clip(gu[:, D_FF:], -SWIGLU_LIMIT, SWIGLU_LIMIT)
        act = gate * (1.0 / (1.0 + jnp.exp(-SWIGLU_ALPHA * gate))) * (up + 1.0)
        y_ref[...] = jnp.dot(act.astype(BF16), wdn_ref[0], preferred_element_type=F32) + bdn_ref[0]


def _experts(tile_expert, n_used, tile_valid, xs, wgu, bgu, wdn, bdn):
    r, d = xs.shape
    tm = TM_EXP
    row = lambda i, te, nu, nv: (jnp.minimum(i, nu[0] - 1), 0)
    exp3 = lambda i, te, nu, nv: (te[i], 0, 0)
    return pl.pallas_call(
        _experts_kernel,
        grid_spec=pltpu.PrefetchScalarGridSpec(
            num_scalar_prefetch=3,
            grid=(r // tm,),
            in_specs=[pl.BlockSpec((tm, d), row),
                      pl.BlockSpec((1, d, 2 * D_FF), exp3),
                      pl.BlockSpec((1, 1, 2 * D_FF), exp3),
                      pl.BlockSpec((1, D_FF, d), exp3),
                      pl.BlockSpec((1, 1, d), exp3)],
            out_specs=pl.BlockSpec((tm, d), row)),
        out_shape=jax.ShapeDtypeStruct((r, d), F32),
        compiler_params=pltpu.CompilerParams(dimension_semantics=("arbitrary",),
                                             vmem_limit_bytes=VMEM_LIMIT),
        name="experts",
    )(tile_expert, n_used, tile_valid, xs, wgu, bgu, wdn, bdn)


def _combine_kernel(pos_ref, y_ref, route_ref, x1_ref, mod_ref, gfin_ref, o_ref, ybuf, sem):
    tm = x1_ref.shape[0]

    def body(t, carry):
        for k in range(TOP_K):
            p = pos_ref[0, 0, TOP_K * t + k]
            pltpu.make_async_copy(y_ref.at[pl.ds(p, 1)], ybuf.at[k, pl.ds(t, 1)], sem).start()
        return carry

    lax.fori_loop(0, tm, body, 0)
    for k in range(TOP_K):
        pltpu.make_async_copy(y_ref.at[pl.ds(0, tm)], ybuf.at[k], sem).wait()

    route = route_ref[...]
    moe = route[:, TOP_K:TOP_K + 1] * ybuf[0]
    for k in range(1, TOP_K):
        moe = moe + route[:, TOP_K + k:TOP_K + k + 1] * ybuf[k]
    m = mod_ref[0]
    xo = x1_ref[...] + m[5:6, :] * moe
    o_ref[...] = _rms(xo) * gfin_ref[...]


def _combine(pos3, y, route, x1, mod6, gfin, n_seq):
    t, d = x1.shape
    tm = TM_ROW
    per_seq = n_seq // tm
    row = lambda i: (i, 0)
    return pl.pallas_call(
        _combine_kernel,
        grid=(t // tm,),
        in_specs=[pl.BlockSpec((1, 1, TOP_K * tm), lambda i: (i, 0, 0), memory_space=pltpu.SMEM),
                  pl.BlockSpec(memory_space=pl.ANY),
                  pl.BlockSpec((tm, LANES), row),
                  pl.BlockSpec((tm, d), row),
                  pl.BlockSpec((1, 6, d), lambda i: (i // per_seq, 0, 0)),
                  pl.BlockSpec((1, d), lambda i: (0, 0))],
        out_specs=pl.BlockSpec((tm, d), row),
        out_shape=jax.ShapeDtypeStruct((t, d), F32),
        scratch_shapes=[pltpu.VMEM((TOP_K, tm, d), F32), pltpu.SemaphoreType.DMA],
        compiler_params=pltpu.CompilerParams(dimension_semantics=("arbitrary",),
                                             vmem_limit_bytes=VMEM_LIMIT),
        name="combine",
    )(pos3, y, route, x1, mod6, gfin)


def kernel(x_prompt, x_sample, c_prompt, c_sample, w_ada, b_ada, g_mix, w_in, w_s, b_s, g_sgu, rpb,
           g_out_a, g_out_b, w_out, g_ffn, w_router, b_router, w_gu, b_gu, w_dn, b_dn, g_final):
    assert w_ada.shape[0] == 1, "single-layer block"
    d = D_MODEL
    groups = [(x_prompt, c_prompt), (x_sample, c_sample)]

    win = w_in[0].astype(BF16)
    ws = w_s[0].astype(BF16)
    bs = jnp.broadcast_to(b_s[0][:, :, None], (N_GROUPS_A, CHUNK, GA)).astype(F32)
    woa = w_out[0, :D_A].astype(BF16)
    wob = w_out[0, D_A:].astype(BF16)
    wr = jnp.pad(w_router[0], ((0, 0), (0, LANES - N_EXPERTS)))
    wrh = wr.astype(BF16)
    wrl = (wr - wrh.astype(F32)).astype(BF16)
    br = jnp.pad(b_router[0], (0, LANES - N_EXPERTS), constant_values=NEG).reshape(1, LANES)
    wgu = w_gu[0].astype(BF16)
    wdn = w_dn[0].astype(BF16)
    bgu = b_gu[0].reshape(N_EXPERTS, 1, 2 * D_FF)
    bdn = b_dn[0].reshape(N_EXPERTS, 1, d)
    bias = _attn_bias_tables(rpb[0])
    row = lambda a: a.reshape(1, -1)

    n_c = sum(c.shape[0] for _, c in groups)
    c_all = jnp.concatenate([c for _, c in groups] + [jnp.zeros((-n_c % 8, d), F32)], axis=0)
    mod_all = _ada(c_all, w_ada[0], row(b_ada[0])).reshape(-1, 6, d)

    per_group = []
    cnt = jnp.zeros((1, LANES), F32)
    b0 = 0
    for x, c in groups:
        b, n, _ = x.shape
        assert n % TM_MIX == 0 and n % Q_TOK == 0
        x2 = x.reshape(b * n, d)
        mod6 = mod_all[b0:b0 + b]
        b0 += b
        an, q, k, v = _mix_in(x2, mod6, row(g_mix[0]), win, ws, bs, row(g_sgu[0]), row(g_out_a[0]), n)
        bn = _attn(q, k, v, bias, row(g_out_b[0]), b, n)
        x1, hf, route, cnt = _mix_out(an, bn, x2, mod6, woa, wob, row(g_ffn[0]), wrh, wrl, br, cnt, n)
        per_group.append((x1, hf, route, mod6, b, n))

    counts = cnt[0, :N_EXPERTS].astype(jnp.int32)
    padded = ((counts + TM_EXP - 1) // TM_EXP) * TM_EXP
    ends = jnp.cumsum(padded)
    starts = ends - padded
    total_slots = sum(bb * nn for *_, bb, nn in per_group) * TOP_K
    n_tiles = (total_slots + N_EXPERTS * (TM_EXP - 1)) // TM_EXP + 1
    n_used = (ends[-1] // TM_EXP).astype(jnp.int32).reshape(1)
    tile_start = jnp.arange(n_tiles, dtype=jnp.int32) * TM_EXP
    tile_expert = jnp.minimum(jnp.sum(tile_start[:, None] >= ends[None, :], axis=1), N_EXPERTS - 1)
    tile_expert = jnp.where(jnp.arange(n_tiles) < n_used[0], tile_expert,
                            tile_expert[jnp.maximum(n_used[0] - 1, 0)]).astype(jnp.int32)

    tile_valid = jnp.clip((starts + counts)[tile_expert] - tile_start, 0, TM_EXP).astype(jnp.int32)

    xs = None
    pos_list = []
    for x1, hf, route, mod6, b, n in per_group:
        idx = route[:, :TOP_K].astype(jnp.int32)
        rank = route[:, 2 * TOP_K:3 * TOP_K].astype(jnp.int32)
        pos = starts[idx] + rank
        pos3 = pos.reshape(-1, 1, TOP_K * TM_ROW)
        pos_list.append(pos3)
        xs = _dispatch(pos3, hf, xs, n_tiles * TM_EXP)

    y = _experts(tile_expert, n_used, tile_valid, xs, wgu, bgu, wdn, bdn)

    outs = []
    for (x1, hf, route, mod6, b, n), pos3 in zip(per_group, pos_list):
        o = _combine(pos3, y, route, x1, mod6, row(g_final), n)
        outs.append(o.reshape(b, n, d))
    return tuple(outs)
```

```python
import functools

import jax
import jax.numpy as jnp
import numpy as np
from jax import lax
from jax.experimental import pallas as pl
from jax.experimental.pallas import tpu as pltpu

F32 = jnp.float32
BF16 = jnp.bfloat16

D_MODEL = 1024
GRID_W = 64
CHUNK = 128
D_A = 512
N_GROUPS_A = 4
GA = D_A // N_GROUPS_A
D_B = 512
N_HEADS = 8
HEAD_DIM = D_B // N_HEADS
WIN_ROWS = 8
WIN_COLS = 16
REL_R = 2 * WIN_ROWS - 1
REL_C = 2 * WIN_COLS - 1
N_EXPERTS = 32
TOP_K = 4
D_FF = D_MODEL
SWIGLU_LIMIT = 7.0
SWIGLU_ALPHA = 1.702
EPS = 1e-6

LANES = 128
NEG = -1e30
Q_ROWS = 4
Q_TOK = Q_ROWS * GRID_W
K_BLOCKS = 3
TM_MIX = 512
TM_EXP = 512
TM_ROW = 256
ROW_UNROLL = 2
VMEM_LIMIT = 56 * 1024 * 1024


def _rms(x):
    return x * lax.rsqrt(jnp.mean(x * x, axis=-1, keepdims=True) + EPS)


def _gelu(x):
    return 0.5 * x * (1.0 + lax.erf(x * 0.7071067811865476))


def _ada_kernel(c_ref, w_ref, b_ref, o_ref):
    c = c_ref[...]
    s = c / (1.0 + jnp.exp(-c))
    o_ref[...] = jnp.dot(s, w_ref[...], preferred_element_type=F32,
                         precision=lax.Precision.HIGHEST) + b_ref[...]


def _ada(c, w, b):
    r, d = c.shape
    n = w.shape[1]
    tn = 1536
    return pl.pallas_call(
        _ada_kernel,
        grid=(n // tn,),
        in_specs=[pl.BlockSpec((r, d), lambda j: (0, 0)),
                  pl.BlockSpec((d, tn), lambda j: (0, j)),
                  pl.BlockSpec((1, tn), lambda j: (0, j))],
        out_specs=pl.BlockSpec((r, tn), lambda j: (0, j)),
        out_shape=jax.ShapeDtypeStruct((r, n), F32),
        name="ada",
    )(c, w, b)


def _mix_in_kernel(x_ref, mod_ref, gmix_ref, win_ref, ws_ref, bs_ref, gsgu_ref, gouta_ref,
                   an_ref, q_ref, k_ref, v_ref):
    x = x_ref[...]
    m = mod_ref[0]
    h = _rms(x) * gmix_ref[...]
    h = h * (1.0 + m[1:2, :]) + m[0:1, :]
    hb = h.astype(BF16)

    u = _gelu(jnp.dot(hb, win_ref[:, 0:D_A], preferred_element_type=F32))
    v = _gelu(jnp.dot(hb, win_ref[:, D_A:2 * D_A], preferred_element_type=F32))
    for j in range(x.shape[0] // CHUNK):
        rows = slice(j * CHUNK, (j + 1) * CHUNK)
        blocks = []
        ssq = None
        for g in range(N_GROUPS_A):
            cols = slice(g * GA, (g + 1) * GA)
            vg = v[rows, cols]
            xc = vg - jnp.mean(vg, axis=-1, keepdims=True)
            vn = xc * lax.rsqrt(jnp.mean(xc * xc, axis=-1, keepdims=True) + EPS) * gsgu_ref[:, cols]
            sp = jnp.dot(ws_ref[g], vn.astype(BF16), preferred_element_type=F32) + bs_ref[g]
            o = u[rows, cols] * sp
            blocks.append(o)
            s = jnp.sum(o * o, axis=-1, keepdims=True)
            ssq = s if ssq is None else ssq + s
        inv = lax.rsqrt(ssq * (1.0 / D_A) + EPS)
        for g in range(N_GROUPS_A):
            cols = slice(g * GA, (g + 1) * GA)
            an_ref[rows, cols] = (blocks[g] * inv * gouta_ref[:, cols]).astype(BF16)

    o0 = 2 * D_A
    q = jnp.dot(hb, win_ref[:, o0:o0 + D_B], preferred_element_type=F32)
    q_ref[...] = (q * (HEAD_DIM ** -0.5)).astype(BF16)
    k_ref[...] = jnp.dot(hb, win_ref[:, o0 + D_B:o0 + 2 * D_B], preferred_element_type=F32).astype(BF16)
    v_ref[...] = jnp.dot(hb, win_ref[:, o0 + 2 * D_B:o0 + 3 * D_B], preferred_element_type=F32).astype(BF16)


def _mix_in(x2, mod6, gmix, win, ws, bs, gsgu, gouta, n_seq):
    t, d = x2.shape
    tm = TM_MIX
    per_seq = n_seq // tm
    row = lambda i: (i, 0)
    const2 = lambda i: (0, 0)
    const3 = lambda i: (0, 0, 0)
    out = jax.ShapeDtypeStruct((t, D_A), BF16)
    return pl.pallas_call(
        _mix_in_kernel,
        grid=(t // tm,),
        in_specs=[pl.BlockSpec((tm, d), row),
                  pl.BlockSpec((1, 6, d), lambda i: (i // per_seq, 0, 0)),
                  pl.BlockSpec((1, d), const2),
                  pl.BlockSpec(win.shape, const2),
                  pl.BlockSpec(ws.shape, const3),
                  pl.BlockSpec(bs.shape, const3),
                  pl.BlockSpec((1, D_A), const2),
                  pl.BlockSpec((1, D_A), const2)],
        out_specs=[pl.BlockSpec((tm, D_A), row)] * 4,
        out_shape=[out] * 4,
        compiler_params=pltpu.CompilerParams(vmem_limit_bytes=VMEM_LIMIT),
        name="mix_in",
    )(x2, mod6, gmix, win, ws, bs, gsgu, gouta)


def _attn_bias_tables(rpb):
    c = np.arange(GRID_W)[:, None]
    kc = np.arange(GRID_W)[None, :]
    cs = np.clip(c - WIN_COLS // 2, 0, GRID_W - WIN_COLS)
    col_ok = (kc >= cs) & (kc < cs + WIN_COLS)
    pick = ((kc - c + (WIN_COLS - 1))[None] == np.arange(REL_C)[:, None, None]) & col_ok[None]
    toe = jnp.einsum("hrj,jck->hrck", rpb.astype(F32), jnp.asarray(pick, F32),
                     precision=lax.Precision.HIGHEST)
    toe = toe + jnp.asarray(np.where(col_ok, 0.0, NEG), F32)
    masked = jnp.full((N_HEADS, GRID_W, GRID_W), NEG, F32)

    rows = 32
    tables = []
    for r0, kb in ((0, 0), (4, 0), (rows - Q_ROWS, rows - K_BLOCKS * Q_ROWS)):
        q_strips = []
        for lr in range(Q_ROWS):
            r = r0 + lr
            rs = min(max(r - WIN_ROWS // 2, 0), rows - WIN_ROWS)
            strip = []
            for lkr in range(K_BLOCKS * Q_ROWS):
                kr = kb + lkr
                strip.append(toe[:, kr - r + (WIN_ROWS - 1)] if rs <= kr < rs + WIN_ROWS else masked)
            q_strips.append(jnp.concatenate(strip, axis=-1))
        tables.append(jnp.concatenate(q_strips, axis=1))
    return jnp.stack(tables)


def _attn_kernel(q_ref, k0_ref, k1_ref, k2_ref, v0_ref, v1_ref, v2_ref, bias_ref, g_ref, o_ref, obuf):
    k_refs = (k0_ref, k1_ref, k2_ref)
    v_refs = (v0_ref, v1_ref, v2_ref)
    nt = (((1,), (1,)), ((), ()))
    for h in range(N_HEADS):
        cols = slice(h * HEAD_DIM, (h + 1) * HEAD_DIM)
        qh = q_ref[:, cols]
        s = [lax.dot_general(qh, k_refs[j][:, cols], nt, preferred_element_type=F32)
             + bias_ref[0, h, :, j * Q_TOK:(j + 1) * Q_TOK] for j in range(K_BLOCKS)]
        mx = jnp.max(jnp.maximum(jnp.maximum(s[0], s[1]), s[2]), axis=-1, keepdims=True)
        p = [jnp.exp(sj - mx) for sj in s]
        l = jnp.sum(p[0] + p[1] + p[2], axis=-1, keepdims=True)
        o = (jnp.dot(p[0].astype(BF16), v_refs[0][:, cols], preferred_element_type=F32)
             + jnp.dot(p[1].astype(BF16), v_refs[1][:, cols], preferred_element_type=F32)
             + jnp.dot(p[2].astype(BF16), v_refs[2][:, cols], preferred_element_type=F32))
        obuf[:, cols] = o / l
    o = obuf[...]
    o_ref[...] = (_rms(o) * g_ref[...]).astype(BF16)


def _attn(q, k, v, bias, goutb, batch, n_seq):
    nq = n_seq // Q_TOK
    assert nq >= K_BLOCKS
    blk = (Q_TOK, D_B)

    def kv_map(j):
        return lambda b, i: (b * nq + jnp.clip(i - 1, 0, nq - K_BLOCKS) + j, 0)

    def bias_map(b, i):
        return (jnp.where(i == 0, 0, jnp.where(i == nq - 1, 2, 1)), 0, 0, 0)

    q_map = lambda b, i: (b * nq + i, 0)
    kv_specs = [pl.BlockSpec(blk, kv_map(j)) for j in range(K_BLOCKS)]
    return pl.pallas_call(
        _attn_kernel,
        grid=(batch, nq),
        in_specs=[pl.BlockSpec(blk, q_map)] + kv_specs + kv_specs
                 + [pl.BlockSpec((1,) + bias.shape[1:], bias_map),
                    pl.BlockSpec((1, D_B), lambda b, i: (0, 0))],
        out_specs=pl.BlockSpec(blk, q_map),
        out_shape=jax.ShapeDtypeStruct(q.shape, BF16),
        scratch_shapes=[pltpu.VMEM(blk, F32)],
        compiler_params=pltpu.CompilerParams(vmem_limit_bytes=VMEM_LIMIT),
        name="attn",
    )(q, k, k, k, v, v, v, bias, goutb)


def _mix_out_kernel(an_ref, bn_ref, x_ref, mod_ref, woa_ref, wob_ref, gffn_ref, wrh_ref, wrl_ref, br_ref,
                    cnt0_ref, x1_ref, hf_ref, route_ref, cnt_ref, run_ref):
    @pl.when(pl.program_id(0) == 0)
    def _():
        run_ref[...] = cnt0_ref[...]

    m = mod_ref[0]
    mixed = (jnp.dot(an_ref[...], woa_ref[...], preferred_element_type=F32)
             + jnp.dot(bn_ref[...], wob_ref[...], preferred_element_type=F32))
    x1 = x_ref[...] + m[2:3, :] * mixed
    x1_ref[...] = x1
    hf = _rms(x1) * gffn_ref[...]
    hf = hf * (1.0 + m[4:5, :]) + m[3:4, :]
    hf_ref[...] = hf

    h_hi = hf.astype(BF16)
    h_lo = (hf - h_hi.astype(F32)).astype(BF16)
    logits = (jnp.dot(h_hi, wrh_ref[...], preferred_element_type=F32)
              + jnp.dot(h_lo, wrh_ref[...], preferred_element_type=F32)
              + jnp.dot(h_hi, wrl_ref[...], preferred_element_type=F32)) + br_ref[...]

    tm = logits.shape[0]
    lane = lax.broadcasted_iota(jnp.int32, (tm, LANES), 1).astype(F32)
    vals, idxs = [], []
    l = logits
    for _ in range(TOP_K):
        mx = jnp.max(l, axis=-1, keepdims=True)
        ik = jnp.min(jnp.where(l == mx, lane, float(LANES)), axis=-1, keepdims=True)
        vals.append(mx)
        idxs.append(ik)
        l = jnp.where(lane == ik, -3e38, l)
    es = [jnp.exp(vk - vals[0]) for vk in vals]
    den = es[0] + es[1] + es[2] + es[3]
    ws = [e / den for e in es]

    hot = jnp.zeros((tm, LANES), F32)
    for ik in idxs:
        hot = hot + jnp.where(lane == ik, 1.0, 0.0)
    ri = lax.broadcasted_iota(jnp.int32, (tm, tm), 0)
    ci = lax.broadcasted_iota(jnp.int32, (tm, tm), 1)
    tri = jnp.where(ri > ci, 1.0, 0.0).astype(BF16)
    before = jnp.dot(tri, hot.astype(BF16), preferred_element_type=F32) + run_ref[...]
    ranks = [jnp.sum(jnp.where(lane == ik, before, 0.0), axis=-1, keepdims=True) for ik in idxs]
    run_ref[...] = run_ref[...] + jnp.sum(hot, axis=0, keepdims=True)
    cnt_ref[...] = run_ref[...]

    route = jnp.zeros((tm, LANES), F32)
    for k in range(TOP_K):
        route = jnp.where(lane == float(k), idxs[k], route)
        route = jnp.where(lane == float(TOP_K + k), ws[k], route)
        route = jnp.where(lane == float(2 * TOP_K + k), ranks[k], route)
    route_ref[...] = route


def _mix_out(an, bn, x2, mod6, woa, wob, gffn, wrh, wrl, br, cnt0, n_seq):
    t, d = x2.shape
    tm = TM_MIX
    per_seq = n_seq // tm
    row = lambda i: (i, 0)
    const2 = lambda i: (0, 0)
    return pl.pallas_call(
        _mix_out_kernel,
        grid=(t // tm,),
        in_specs=[pl.BlockSpec((tm, D_A), row),
                  pl.BlockSpec((tm, D_B), row),
                  pl.BlockSpec((tm, d), row),
                  pl.BlockSpec((1, 6, d), lambda i: (i // per_seq, 0, 0)),
                  pl.BlockSpec(woa.shape, const2),
                  pl.BlockSpec(wob.shape, const2),
                  pl.BlockSpec((1, d), const2),
                  pl.BlockSpec(wrh.shape, const2),
                  pl.BlockSpec(wrl.shape, const2),
                  pl.BlockSpec((1, LANES), const2),
                  pl.BlockSpec((1, LANES), const2)],
        out_specs=[pl.BlockSpec((tm, d), row),
                   pl.BlockSpec((tm, d), row),
                   pl.BlockSpec((tm, LANES), row),
                   pl.BlockSpec((1, LANES), const2)],
        out_shape=[jax.ShapeDtypeStruct((t, d), F32),
                   jax.ShapeDtypeStruct((t, d), F32),
                   jax.ShapeDtypeStruct((t, LANES), F32),
                   jax.ShapeDtypeStruct((1, LANES), F32)],
        scratch_shapes=[pltpu.VMEM((1, LANES), F32)],
        compiler_params=pltpu.CompilerParams(dimension_semantics=("arbitrary",),
                                             vmem_limit_bytes=VMEM_LIMIT),
        name="mix_out",
    )(an, bn, x2, mod6, woa, wob, gffn, wrh, wrl, br, cnt0)


def _dispatch_kernel(pos_ref, hf_ref, *rest):
    xs_ref, sem = rest[-2], rest[-1]
    tm = hf_ref.shape[0]

    def body(g, carry):
        for u in range(ROW_UNROLL):
            t = g * ROW_UNROLL + u
            for k in range(TOP_K):
                p = pos_ref[0, 0, TOP_K * t + k]
                pltpu.make_async_copy(hf_ref.at[pl.ds(t, 1)], xs_ref.at[pl.ds(p, 1)],
                                      sem).start(priority=k % 2)
        return carry

    lax.fori_loop(0, tm // ROW_UNROLL, body, 0)
    for _ in range(TOP_K):
        pltpu.make_async_copy(hf_ref, xs_ref.at[pl.ds(0, tm)], sem).wait()


def _dispatch(pos3, hf, xs_prev, n_rows):
    t, d = hf.shape
    tm = TM_ROW
    in_specs = [pl.BlockSpec((1, 1, TOP_K * tm), lambda i: (i, 0, 0), memory_space=pltpu.SMEM),
                pl.BlockSpec((tm, d), lambda i: (i, 0))]
    args = [pos3, hf]
    aliases = {}
    if xs_prev is not None:
        in_specs.append(pl.BlockSpec(memory_space=pl.ANY))
        args.append(xs_prev)
        aliases = {2: 0}
    return pl.pallas_call(
        _dispatch_kernel,
        grid=(t // tm,),
        in_specs=in_specs,
        out_specs=pl.BlockSpec(memory_space=pl.ANY),
        out_shape=jax.ShapeDtypeStruct((n_rows, d), F32),
        scratch_shapes=[pltpu.SemaphoreType.DMA],
        input_output_aliases=aliases,
        compiler_params=pltpu.CompilerParams(dimension_semantics=("arbitrary",),
                                             has_side_effects=True),
        name="dispatch",
    )(*args)


def _experts_kernel(te_ref, nu_ref, nv_ref, xs_ref, wgu_ref, bgu_ref, wdn_ref, bdn_ref, y_ref):
    i = pl.program_id(0)

    @pl.when(i < nu_ref[0])
    def _():
        rid = lax.broadcasted_iota(jnp.int32, xs_ref.shape, 0)
        xb = jnp.where(rid < nv_ref[i], xs_ref[...], 0.0).astype(BF16)
        gu = jnp.dot(xb, wgu_ref[0], preferred_element_type=F32) + bgu_ref[0]
        gate = jnp.minimum(gu[:, :D_FF], SWIGLU_LIMIT)
        up = jnp.clip(gu[:, D_FF:], -SWIGLU_LIMIT, SWIGLU_LIMIT)
        act = gate * (1.0 / (1.0 + jnp.exp(-SWIGLU_ALPHA * gate))) * (up + 1.0)
        y_ref[...] = jnp.dot(act.astype(BF16), wdn_ref[0], preferred_element_type=F32) + bdn_ref[0]


def _experts(tile_expert, n_used, tile_valid, xs, wgu, bgu, wdn, bdn):
    r, d = xs.shape
    tm = TM_EXP
    row = lambda i, te, nu, nv: (jnp.minimum(i, nu[0] - 1), 0)
    exp3 = lambda i, te, nu, nv: (te[i], 0, 0)
    return pl.pallas_call(
        _experts_kernel,
        grid_spec=pltpu.PrefetchScalarGridSpec(
            num_scalar_prefetch=3,
            grid=(r // tm,),
            in_specs=[pl.BlockSpec((tm, d), row),
                      pl.BlockSpec((1, d, 2 * D_FF), exp3),
                      pl.BlockSpec((1, 1, 2 * D_FF), exp3),
                      pl.BlockSpec((1, D_FF, d), exp3),
                      pl.BlockSpec((1, 1, d), exp3)],
            out_specs=pl.BlockSpec((tm, d), row)),
        out_shape=jax.ShapeDtypeStruct((r, d), F32),
        compiler_params=pltpu.CompilerParams(dimension_semantics=("arbitrary",),
                                             vmem_limit_bytes=VMEM_LIMIT),
        name="experts",
    )(tile_expert, n_used, tile_valid, xs, wgu, bgu, wdn, bdn)


def _combine_kernel(n_steps, pos_ref, posn_ref, y_ref, route_ref, x1_ref, mod_ref, gfin_ref, o_ref,
                    ybuf, sems):
    tm = x1_ref.shape[0]
    i = pl.program_id(0)
    slot = i % 2

    def gather(p_ref, s):
        def body(g, carry):
            for u in range(ROW_UNROLL):
                t = g * ROW_UNROLL + u
                for k in range(TOP_K):
                    p = p_ref[0, 0, TOP_K * t + k]
                    pltpu.make_async_copy(y_ref.at[pl.ds(p, 1)], ybuf.at[s, k, pl.ds(t, 1)],
                                          sems.at[s]).start(priority=k % 2)
            return carry

        lax.fori_loop(0, tm // ROW_UNROLL, body, 0)

    @pl.when(i == 0)
    def _():
        gather(pos_ref, 0)

    @pl.when(i + 1 < n_steps)
    def _():
        gather(posn_ref, 1 - slot)

    for k in range(TOP_K):
        pltpu.make_async_copy(y_ref.at[pl.ds(0, tm)], ybuf.at[slot, k], sems.at[slot]).wait()

    route = route_ref[...]
    moe = route[:, TOP_K:TOP_K + 1] * ybuf[slot, 0]
    for k in range(1, TOP_K):
        moe = moe + route[:, TOP_K + k:TOP_K + k + 1] * ybuf[slot, k]
    m = mod_ref[0]
    xo = x1_ref[...] + m[5:6, :] * moe
    o_ref[...] = _rms(xo) * gfin_ref[...]


def _combine(pos3, y, route, x1, mod6, gfin, n_seq):
    t, d = x1.shape
    tm = TM_ROW
    per_seq = n_seq // tm
    row = lambda i: (i, 0)
    n_steps = t // tm
    last = n_steps - 1
    return pl.pallas_call(
        functools.partial(_combine_kernel, n_steps),
        grid=(n_steps,),
        in_specs=[pl.BlockSpec((1, 1, TOP_K * tm), lambda i: (i, 0, 0), memory_space=pltpu.SMEM),
                  pl.BlockSpec((1, 1, TOP_K * tm), lambda i: (jnp.minimum(i + 1, last), 0, 0),
                               memory_space=pltpu.SMEM),
                  pl.BlockSpec(memory_space=pl.ANY),
                  pl.BlockSpec((tm, LANES), row),
                  pl.BlockSpec((tm, d), row),
                  pl.BlockSpec((1, 6, d), lambda i: (i // per_seq, 0, 0)),
                  pl.BlockSpec((1, d), lambda i: (0, 0))],
        out_specs=pl.BlockSpec((tm, d), row),
        out_shape=jax.ShapeDtypeStruct((t, d), F32),
        scratch_shapes=[pltpu.VMEM((2, TOP_K, tm, d), F32), pltpu.SemaphoreType.DMA((2,))],
        compiler_params=pltpu.CompilerParams(dimension_semantics=("arbitrary",),
                                             vmem_limit_bytes=VMEM_LIMIT),
        name="combine",
    )(pos3, pos3, y, route, x1, mod6, gfin)


def kernel(x_prompt, x_sample, c_prompt, c_sample, w_ada, b_ada, g_mix, w_in, w_s, b_s, g_sgu, rpb,
           g_out_a, g_out_b, w_out, g_ffn, w_router, b_router, w_gu, b_gu, w_dn, b_dn, g_final):
    assert w_ada.shape[0] == 1, "single-layer block"
    d = D_MODEL
    groups = [(x_prompt, c_prompt), (x_sample, c_sample)]

    win = w_in[0].astype(BF16)
    ws = w_s[0].astype(BF16)
    bs = jnp.broadcast_to(b_s[0][:, :, None], (N_GROUPS_A, CHUNK, GA)).astype(F32)
    woa = w_out[0, :D_A].astype(BF16)
    wob = w_out[0, D_A:].astype(BF16)
    wr = jnp.pad(w_router[0], ((0, 0), (0, LANES - N_EXPERTS)))
    wrh = wr.astype(BF16)
    wrl = (wr - wrh.astype(F32)).astype(BF16)
    br = jnp.pad(b_router[0], (0, LANES - N_EXPERTS), constant_values=NEG).reshape(1, LANES)
    wgu = w_gu[0].astype(BF16)
    wdn = w_dn[0].astype(BF16)
    bgu = b_gu[0].reshape(N_EXPERTS, 1, 2 * D_FF)
    bdn = b_dn[0].reshape(N_EXPERTS, 1, d)
    bias = _attn_bias_tables(rpb[0])
    row = lambda a: a.reshape(1, -1)

    n_c = sum(c.shape[0] for _, c in groups)
    c_all = jnp.concatenate([c for _, c in groups] + [jnp.zeros((-n_c % 8, d), F32)], axis=0)
    mod_all = _ada(c_all, w_ada[0], row(b_ada[0])).reshape(-1, 6, d)

    per_group = []
    cnt = jnp.zeros((1, LANES), F32)
    b0 = 0
    for x, c in groups:
        b, n, _ = x.shape
        assert n % TM_MIX == 0 and n % Q_TOK == 0
        x2 = x.reshape(b * n, d)
        mod6 = mod_all[b0:b0 + b]
        b0 += b
        an, q, k, v = _mix_in(x2, mod6, row(g_mix[0]), win, ws, bs, row(g_sgu[0]), row(g_out_a[0]), n)
        bn = _attn(q, k, v, bias, row(g_out_b[0]), b, n)
        x1, hf, route, cnt = _mix_out(an, bn, x2, mod6, woa, wob, row(g_ffn[0]), wrh, wrl, br, cnt, n)
        per_group.append((x1, hf, route, mod6, b, n))

    counts = cnt[0, :N_EXPERTS].astype(jnp.int32)
    padded = ((counts + TM_EXP - 1) // TM_EXP) * TM_EXP
    ends = jnp.cumsum(padded)
    starts = ends - padded
    total_slots = sum(bb * nn for *_, bb, nn in per_group) * TOP_K
    n_tiles = (total_slots + N_EXPERTS * (TM_EXP - 1)) // TM_EXP + 1
    n_used = (ends[-1] // TM_EXP).astype(jnp.int32).reshape(1)
    tile_start = jnp.arange(n_tiles, dtype=jnp.int32) * TM_EXP
    tile_expert = jnp.minimum(jnp.sum(tile_start[:, None] >= ends[None, :], axis=1), N_EXPERTS - 1)
    tile_expert = jnp.where(jnp.arange(n_tiles) < n_used[0], tile_expert,
                            tile_expert[jnp.maximum(n_used[0] - 1, 0)]).astype(jnp.int32)

    tile_valid = jnp.clip((starts + counts)[tile_expert] - tile_start, 0, TM_EXP).astype(jnp.int32)

    xs = None
    pos_list = []
    for x1, hf, route, mod6, b, n in per_group:
        idx = route[:, :TOP_K].astype(jnp.int32)
        rank = route[:, 2 * TOP_K:3 * TOP_K].astype(jnp.int32)
        pos = starts[idx] + rank
        pos3 = pos.reshape(-1, 1, TOP_K * TM_ROW)
        pos_list.append(pos3)
        xs = _dispatch(pos3, hf, xs, n_tiles * TM_EXP)

    y = _experts(tile_expert, n_used, tile_valid, xs, wgu, bgu, wdn, bdn)

    outs = []
    for (x1, hf, route, mod6, b, n), pos3 in zip(per_group, pos_list):
        o = _combine(pos3, y, route, x1, mod6, row(g_final), n)
        outs.append(o.reshape(b, n, d))
    return tuple(outs)
```

```python
import functools

import jax
import jax.numpy as jnp
import numpy as np
from jax import lax
from jax.experimental import pallas as pl
from jax.experimental.pallas import tpu as pltpu
from jax.experimental.pallas import tpu_sc as plsc

F32 = jnp.float32
BF16 = jnp.bfloat16

D_MODEL = 1024
GRID_W = 64
CHUNK = 128
D_A = 512
N_GROUPS_A = 4
GA = D_A // N_GROUPS_A
D_B = 512
N_HEADS = 8
HEAD_DIM = D_B // N_HEADS
WIN_ROWS = 8
WIN_COLS = 16
REL_R = 2 * WIN_ROWS - 1
REL_C = 2 * WIN_COLS - 1
N_EXPERTS = 32
TOP_K = 4
D_FF = D_MODEL
SWIGLU_LIMIT = 7.0
SWIGLU_ALPHA = 1.702
EPS = 1e-6

LANES = 128
NEG = -1e30
Q_ROWS = 4
Q_TOK = Q_ROWS * GRID_W
K_BLOCKS = 3
TM_MIX = 512
TM_EXP = 512
TM_COMB = 256
ROUTE_CHUNK = 16384
ROW_SUB = 8
SC_CORES = 2
SC_SUBCORES = 16
SC_ROWS = 32
VMEM_LIMIT = 56 * 1024 * 1024


def _rms(x):
    return x * lax.rsqrt(jnp.mean(x * x, axis=-1, keepdims=True) + EPS)


def _gelu(x):
    return 0.5 * x * (1.0 + lax.erf(x * 0.7071067811865476))


def _ada_kernel(c_ref, w_ref, b_ref, o_ref):
    c = c_ref[...]
    s = c / (1.0 + jnp.exp(-c))
    o_ref[...] = jnp.dot(s, w_ref[...], preferred_element_type=F32,
                         precision=lax.Precision.HIGHEST) + b_ref[...]


def _ada(c, w, b):
    r, d = c.shape
    n = w.shape[1]
    tn = 1536
    return pl.pallas_call(
        _ada_kernel,
        grid=(n // tn,),
        in_specs=[pl.BlockSpec((r, d), lambda j: (0, 0)),
                  pl.BlockSpec((d, tn), lambda j: (0, j)),
                  pl.BlockSpec((1, tn), lambda j: (0, j))],
        out_specs=pl.BlockSpec((r, tn), lambda j: (0, j)),
        out_shape=jax.ShapeDtypeStruct((r, n), F32),
        name="ada",
    )(c, w, b)


def _mix_in_kernel(x_ref, mod_ref, gmix_ref, win_ref, ws_ref, bs_ref, gsgu_ref, gouta_ref,
                   an_ref, q_ref, k_ref, v_ref):
    x = x_ref[...]
    m = mod_ref[0]
    h = _rms(x) * gmix_ref[...]
    h = h * (1.0 + m[1:2, :]) + m[0:1, :]
    hb = h.astype(BF16)

    u = _gelu(jnp.dot(hb, win_ref[:, 0:D_A], preferred_element_type=F32))
    v = _gelu(jnp.dot(hb, win_ref[:, D_A:2 * D_A], preferred_element_type=F32))
    for j in range(x.shape[0] // CHUNK):
        rows = slice(j * CHUNK, (j + 1) * CHUNK)
        blocks = []
        ssq = None
        for g in range(N_GROUPS_A):
            cols = slice(g * GA, (g + 1) * GA)
            vg = v[rows, cols]
            xc = vg - jnp.mean(vg, axis=-1, keepdims=True)
            vn = xc * lax.rsqrt(jnp.mean(xc * xc, axis=-1, keepdims=True) + EPS) * gsgu_ref[:, cols]
            sp = jnp.dot(ws_ref[g], vn.astype(BF16), preferred_element_type=F32) + bs_ref[g]
            o = u[rows, cols] * sp
            blocks.append(o)
            s = jnp.sum(o * o, axis=-1, keepdims=True)
            ssq = s if ssq is None else ssq + s
        inv = lax.rsqrt(ssq * (1.0 / D_A) + EPS)
        for g in range(N_GROUPS_A):
            cols = slice(g * GA, (g + 1) * GA)
            an_ref[rows, cols] = (blocks[g] * inv * gouta_ref[:, cols]).astype(BF16)

    o0 = 2 * D_A
    q = jnp.dot(hb, win_ref[:, o0:o0 + D_B], preferred_element_type=F32)
    q_ref[...] = (q * (HEAD_DIM ** -0.5)).astype(BF16)
    k_ref[...] = jnp.dot(hb, win_ref[:, o0 + D_B:o0 + 2 * D_B], preferred_element_type=F32).astype(BF16)
    v_ref[...] = jnp.dot(hb, win_ref[:, o0 + 2 * D_B:o0 + 3 * D_B], preferred_element_type=F32).astype(BF16)


def _mix_in(x2, mod6, gmix, win, ws, bs, gsgu, gouta, n_seq):
    t, d = x2.shape
    tm = TM_MIX
    per_seq = n_seq // tm
    row = lambda i: (i, 0)
    const2 = lambda i: (0, 0)
    const3 = lambda i: (0, 0, 0)
    out = jax.ShapeDtypeStruct((t, D_A), BF16)
    return pl.pallas_call(
        _mix_in_kernel,
        grid=(t // tm,),
        in_specs=[pl.BlockSpec((tm, d), row),
                  pl.BlockSpec((1, 6, d), lambda i: (i // per_seq, 0, 0)),
                  pl.BlockSpec((1, d), const2),
                  pl.BlockSpec(win.shape, const2),
                  pl.BlockSpec(ws.shape, const3),
                  pl.BlockSpec(bs.shape, const3),
                  pl.BlockSpec((1, D_A), const2),
                  pl.BlockSpec((1, D_A), const2)],
        out_specs=[pl.BlockSpec((tm, D_A), row)] * 4,
        out_shape=[out] * 4,
        compiler_params=pltpu.CompilerParams(vmem_limit_bytes=VMEM_LIMIT),
        name="mix_in",
    )(x2, mod6, gmix, win, ws, bs, gsgu, gouta)


def _attn_bias_tables(rpb):
    c = np.arange(GRID_W)[:, None]
    kc = np.arange(GRID_W)[None, :]
    cs = np.clip(c - WIN_COLS // 2, 0, GRID_W - WIN_COLS)
    col_ok = (kc >= cs) & (kc < cs + WIN_COLS)
    pick = ((kc - c + (WIN_COLS - 1))[None] == np.arange(REL_C)[:, None, None]) & col_ok[None]
    toe = jnp.einsum("hrj,jck->hrck", rpb.astype(F32), jnp.asarray(pick, F32),
                     precision=lax.Precision.HIGHEST)
    toe = toe + jnp.asarray(np.where(col_ok, 0.0, NEG), F32)
    masked = jnp.full((N_HEADS, GRID_W, GRID_W), NEG, F32)

    rows = 32
    tables = []
    for r0, kb in ((0, 0), (4, 0), (rows - Q_ROWS, rows - K_BLOCKS * Q_ROWS)):
        q_strips = []
        for lr in range(Q_ROWS):
            r = r0 + lr
            rs = min(max(r - WIN_ROWS // 2, 0), rows - WIN_ROWS)
            strip = []
            for lkr in range(K_BLOCKS * Q_ROWS):
                kr = kb + lkr
                strip.append(toe[:, kr - r + (WIN_ROWS - 1)] if rs <= kr < rs + WIN_ROWS else masked)
            q_strips.append(jnp.concatenate(strip, axis=-1))
        tables.append(jnp.concatenate(q_strips, axis=1))
    return jnp.stack(tables)


def _attn_kernel(q_ref, k0_ref, k1_ref, k2_ref, v0_ref, v1_ref, v2_ref, bias_ref, g_ref, o_ref, obuf):
    k_refs = (k0_ref, k1_ref, k2_ref)
    v_refs = (v0_ref, v1_ref, v2_ref)
    nt = (((1,), (1,)), ((), ()))
    for h in range(N_HEADS):
        cols = slice(h * HEAD_DIM, (h + 1) * HEAD_DIM)
        qh = q_ref[:, cols]
        s = [lax.dot_general(qh, k_refs[j][:, cols], nt, preferred_element_type=F32)
             + bias_ref[0, h, :, j * Q_TOK:(j + 1) * Q_TOK] for j in range(K_BLOCKS)]
        mx = jnp.max(jnp.maximum(jnp.maximum(s[0], s[1]), s[2]), axis=-1, keepdims=True)
        p = [jnp.exp(sj - mx) for sj in s]
        l = jnp.sum(p[0] + p[1] + p[2], axis=-1, keepdims=True)
        o = (jnp.dot(p[0].astype(BF16), v_refs[0][:, cols], preferred_element_type=F32)
             + jnp.dot(p[1].astype(BF16), v_refs[1][:, cols], preferred_element_type=F32)
             + jnp.dot(p[2].astype(BF16), v_refs[2][:, cols], preferred_element_type=F32))
        obuf[:, cols] = o / l
    o = obuf[...]
    o_ref[...] = (_rms(o) * g_ref[...]).astype(BF16)


def _attn(q, k, v, bias, goutb, batch, n_seq):
    nq = n_seq // Q_TOK
    assert nq >= K_BLOCKS
    blk = (Q_TOK, D_B)

    def kv_map(j):
        return lambda b, i: (b * nq + jnp.clip(i - 1, 0, nq - K_BLOCKS) + j, 0)

    def bias_map(b, i):
        return (jnp.where(i == 0, 0, jnp.where(i == nq - 1, 2, 1)), 0, 0, 0)

    q_map = lambda b, i: (b * nq + i, 0)
    kv_specs = [pl.BlockSpec(blk, kv_map(j)) for j in range(K_BLOCKS)]
    return pl.pallas_call(
        _attn_kernel,
        grid=(batch, nq),
        in_specs=[pl.BlockSpec(blk, q_map)] + kv_specs + kv_specs
                 + [pl.BlockSpec((1,) + bias.shape[1:], bias_map),
                    pl.BlockSpec((1, D_B), lambda b, i: (0, 0))],
        out_specs=pl.BlockSpec(blk, q_map),
        out_shape=jax.ShapeDtypeStruct(q.shape, BF16),
        scratch_shapes=[pltpu.VMEM(blk, F32)],
        compiler_params=pltpu.CompilerParams(vmem_limit_bytes=VMEM_LIMIT),
        name="attn",
    )(q, k, k, k, v, v, v, bias, goutb)


def _mix_out_kernel(steps_per_chunk, an_ref, bn_ref, x_ref, mod_ref, woa_ref, wob_ref, gffn_ref, wrh_ref,
                    wrl_ref, br_ref, x1_ref, hf_ref, route_ref, cnt_ref, run_ref):
    @pl.when(pl.program_id(0) % steps_per_chunk == 0)
    def _():
        run_ref[...] = jnp.zeros_like(run_ref)

    m = mod_ref[0]
    mixed = (jnp.dot(an_ref[...], woa_ref[...], preferred_element_type=F32)
             + jnp.dot(bn_ref[...], wob_ref[...], preferred_element_type=F32))
    x1 = x_ref[...] + m[2:3, :] * mixed
    x1_ref[...] = x1
    hf = _rms(x1) * gffn_ref[...]
    hf = hf * (1.0 + m[4:5, :]) + m[3:4, :]
    hf_ref[...] = hf.reshape(hf_ref.shape)

    h_hi = hf.astype(BF16)
    h_lo = (hf - h_hi.astype(F32)).astype(BF16)
    logits = (jnp.dot(h_hi, wrh_ref[...], preferred_element_type=F32)
              + jnp.dot(h_lo, wrh_ref[...], preferred_element_type=F32)
              + jnp.dot(h_hi, wrl_ref[...], preferred_element_type=F32)) + br_ref[...]

    tm = logits.shape[0]
    lane = lax.broadcasted_iota(jnp.int32, (tm, LANES), 1).astype(F32)
    vals, idxs = [], []
    l = logits
    for _ in range(TOP_K):
        mx = jnp.max(l, axis=-1, keepdims=True)
        ik = jnp.min(jnp.where(l == mx, lane, float(LANES)), axis=-1, keepdims=True)
        vals.append(mx)
        idxs.append(ik)
        l = jnp.where(lane == ik, -3e38, l)
    es = [jnp.exp(vk - vals[0]) for vk in vals]
    den = es[0] + es[1] + es[2] + es[3]
    ws = [e / den for e in es]

    hot = jnp.zeros((tm, LANES), F32)
    for ik in idxs:
        hot = hot + jnp.where(lane == ik, 1.0, 0.0)
    ri = lax.broadcasted_iota(jnp.int32, (tm, tm), 0)
    ci = lax.broadcasted_iota(jnp.int32, (tm, tm), 1)
    tri = jnp.where(ri > ci, 1.0, 0.0).astype(BF16)
    before = jnp.dot(tri, hot.astype(BF16), preferred_element_type=F32) + run_ref[...]
    ranks = [jnp.sum(jnp.where(lane == ik, before, 0.0), axis=-1, keepdims=True) for ik in idxs]
    run_ref[...] = run_ref[...] + jnp.sum(hot, axis=0, keepdims=True)
    cnt_ref[0] = run_ref[...]

    route = jnp.zeros((tm, LANES), F32)
    for k in range(TOP_K):
        route = jnp.where(lane == float(k), idxs[k], route)
        route = jnp.where(lane == float(TOP_K + k), ws[k], route)
        route = jnp.where(lane == float(2 * TOP_K + k), ranks[k], route)
    route_ref[...] = route


def _mix_out(an, bn, x2, mod6, woa, wob, gffn, wrh, wrl, br, n_seq):
    t, d = x2.shape
    tm = TM_MIX
    per_seq = n_seq // tm
    spc = ROUTE_CHUNK // tm
    row = lambda i: (i, 0)
    const2 = lambda i: (0, 0)
    return pl.pallas_call(
        functools.partial(_mix_out_kernel, spc),
        grid=(t // tm,),
        in_specs=[pl.BlockSpec((tm, D_A), row),
                  pl.BlockSpec((tm, D_B), row),
                  pl.BlockSpec((tm, d), row),
                  pl.BlockSpec((1, 6, d), lambda i: (i // per_seq, 0, 0)),
                  pl.BlockSpec(woa.shape, const2),
                  pl.BlockSpec(wob.shape, const2),
                  pl.BlockSpec((1, d), const2),
                  pl.BlockSpec(wrh.shape, const2),
                  pl.BlockSpec(wrl.shape, const2),
                  pl.BlockSpec((1, LANES), const2)],
        out_specs=[pl.BlockSpec((tm, d), row),
                   pl.BlockSpec((tm, ROW_SUB, LANES), lambda i: (i, 0, 0)),
                   pl.BlockSpec((tm, LANES), row),
                   pl.BlockSpec((1, 1, LANES), lambda i: (i // spc, 0, 0))],
        out_shape=[jax.ShapeDtypeStruct((t, d), F32),
                   jax.ShapeDtypeStruct((t, ROW_SUB, LANES), F32),
                   jax.ShapeDtypeStruct((t, LANES), F32),
                   jax.ShapeDtypeStruct((t // ROUTE_CHUNK, 1, LANES), F32)],
        scratch_shapes=[pltpu.VMEM((1, LANES), F32)],
        compiler_params=pltpu.CompilerParams(dimension_semantics=("arbitrary",),
                                             vmem_limit_bytes=VMEM_LIMIT),
        name="mix_out",
    )(an, bn, x2, mod6, woa, wob, gffn, wrh, wrl, br)


def _sc_mesh():
    return plsc.VectorSubcoreMesh(core_axis_name="c", subcore_axis_name="s")


def _sc_worker_blocks(n_tok):
    workers = SC_CORES * SC_SUBCORES
    assert n_tok % (workers * SC_ROWS) == 0
    return n_tok // (workers * SC_ROWS)


def _dispatch(hf3, posb, tok0, n_tok, n_rows):
    nblk = _sc_worker_blocks(n_tok)

    @functools.partial(
        pl.kernel, mesh=_sc_mesh(),
        out_type=jax.ShapeDtypeStruct((n_rows, ROW_SUB, LANES), F32),
        scratch_types=[pltpu.VMEM((TOP_K, SC_ROWS), jnp.int32),
                       pltpu.VMEM((SC_ROWS, ROW_SUB, LANES), F32),
                       pltpu.SemaphoreType.DMA],
        compiler_params=pltpu.CompilerParams(use_tc_tiling_on_sc=True))
    def dispatch(hf_hbm, pos_hbm, xs_hbm, idx_v, rows_v, sem):
        wid = lax.axis_index("s") * SC_CORES + lax.axis_index("c")

        @pl.loop(0, nblk)
        def _(b):
            blk = wid * nblk + b
            pltpu.sync_copy(pos_hbm.at[blk], idx_v)
            pltpu.sync_copy(hf_hbm.at[pl.ds(tok0 + blk * SC_ROWS, SC_ROWS)], rows_v)
            copies = [pltpu.async_copy(rows_v, xs_hbm.at[idx_v.at[k]], sem) for k in range(TOP_K)]
            for cp in copies:
                cp.wait()

    return dispatch(hf3, posb)


def _collect(y3, posb, n_tok):
    nblk = _sc_worker_blocks(n_tok)

    @functools.partial(
        pl.kernel, mesh=_sc_mesh(),
        out_type=jax.ShapeDtypeStruct((TOP_K, n_tok, ROW_SUB, LANES), F32),
        scratch_types=[pltpu.VMEM((TOP_K, SC_ROWS), jnp.int32),
                       pltpu.VMEM((SC_ROWS, ROW_SUB, LANES), F32),
                       pltpu.SemaphoreType.DMA],
        compiler_params=pltpu.CompilerParams(use_tc_tiling_on_sc=True))
    def collect(y_hbm, pos_hbm, g_hbm, idx_v, rows_v, sem):
        wid = lax.axis_index("s") * SC_CORES + lax.axis_index("c")

        @pl.loop(0, nblk)
        def _(b):
            blk = wid * nblk + b
            pltpu.sync_copy(pos_hbm.at[blk], idx_v)
            for k in range(TOP_K):
                pltpu.async_copy(y_hbm.at[idx_v.at[k]], rows_v, sem).wait()
                pltpu.sync_copy(rows_v, g_hbm.at[k, pl.ds(blk * SC_ROWS, SC_ROWS)])

    return collect(y3, posb)


def _experts_kernel(te_ref, nu_ref, nv_ref, xs_ref, wgu_ref, bgu_ref, wdn_ref, bdn_ref, y_ref):
    i = pl.program_id(0)

    @pl.when(i < nu_ref[0])
    def _():
        tm = xs_ref.shape[0]
        x = xs_ref[...].reshape(tm, D_MODEL)
        rid = lax.broadcasted_iota(jnp.int32, x.shape, 0)
        xb = jnp.where(rid < nv_ref[i], x, 0.0).astype(BF16)
        gu = jnp.dot(xb, wgu_ref[0], preferred_element_type=F32) + bgu_ref[0]
        gate = jnp.minimum(gu[:, :D_FF], SWIGLU_LIMIT)
        up = jnp.clip(gu[:, D_FF:], -SWIGLU_LIMIT, SWIGLU_LIMIT)
        act = gate * (1.0 / (1.0 + jnp.exp(-SWIGLU_ALPHA * gate))) * (up + 1.0)
        y = jnp.dot(act.astype(BF16), wdn_ref[0], preferred_element_type=F32) + bdn_ref[0]
        y_ref[...] = y.reshape(y_ref.shape)


def _experts(tile_expert, n_used, tile_valid, xs3, wgu, bgu, wdn, bdn):
    r = xs3.shape[0]
    d = D_MODEL
    tm = TM_EXP
    row = lambda i, te, nu, nv: (jnp.minimum(i, nu[0] - 1), 0, 0)
    exp3 = lambda i, te, nu, nv: (te[i], 0, 0)
    return pl.pallas_call(
        _experts_kernel,
        grid_spec=pltpu.PrefetchScalarGridSpec(
            num_scalar_prefetch=3,
            grid=(r // tm,),
            in_specs=[pl.BlockSpec((tm, ROW_SUB, LANES), row),
                      pl.BlockSpec((1, d, 2 * D_FF), exp3),
                      pl.BlockSpec((1, 1, 2 * D_FF), exp3),
                      pl.BlockSpec((1, D_FF, d), exp3),
                      pl.BlockSpec((1, 1, d), exp3)],
            out_specs=pl.BlockSpec((tm, ROW_SUB, LANES), row)),
        out_shape=jax.ShapeDtypeStruct((r, ROW_SUB, LANES), F32),
        compiler_params=pltpu.CompilerParams(dimension_semantics=("arbitrary",),
                                             vmem_limit_bytes=VMEM_LIMIT),
        name="experts",
    )(tile_expert, n_used, tile_valid, xs3, wgu, bgu, wdn, bdn)


def _combine_kernel(g_ref, route_ref, x1_ref, mod_ref, gfin_ref, *rest):
    o_ref = rest[-1]
    tm = x1_ref.shape[0]
    route = route_ref[...]
    moe = route[:, TOP_K:TOP_K + 1] * g_ref[0].reshape(tm, D_MODEL)
    for k in range(1, TOP_K):
        moe = moe + route[:, TOP_K + k:TOP_K + k + 1] * g_ref[k].reshape(tm, D_MODEL)
    m = mod_ref[0]
    xo = x1_ref[...] + m[5:6, :] * moe
    o_ref[...] = _rms(xo) * gfin_ref[...]


def _combine(g, route, x1, mod6, gfin, out_prev, tok0, n_seq):
    t, d = x1.shape
    n_tok = g.shape[1]
    tm = TM_COMB
    blk0 = tok0 // tm
    per_seq = n_seq // tm
    row = lambda i: (blk0 + i, 0)
    in_specs = [pl.BlockSpec((TOP_K, tm, ROW_SUB, LANES), lambda i: (0, i, 0, 0)),
                pl.BlockSpec((tm, LANES), row),
                pl.BlockSpec((tm, d), row),
                pl.BlockSpec((1, 6, d), lambda i: ((blk0 + i) // per_seq, 0, 0)),
                pl.BlockSpec((1, d), lambda i: (0, 0))]
    args = [g, route, x1, mod6, gfin]
    aliases = {}
    if out_prev is not None:
        in_specs.append(pl.BlockSpec(memory_space=pl.ANY))
        args.append(out_prev)
        aliases = {len(args) - 1: 0}
    return pl.pallas_call(
        _combine_kernel,
        grid=(n_tok // tm,),
        in_specs=in_specs,
        out_specs=pl.BlockSpec((tm, d), row),
        out_shape=jax.ShapeDtypeStruct((t, d), F32),
        input_output_aliases=aliases,
        compiler_params=pltpu.CompilerParams(vmem_limit_bytes=VMEM_LIMIT),
        name="combine",
    )(*args)


def kernel(x_prompt, x_sample, c_prompt, c_sample, w_ada, b_ada, g_mix, w_in, w_s, b_s, g_sgu, rpb,
           g_out_a, g_out_b, w_out, g_ffn, w_router, b_router, w_gu, b_gu, w_dn, b_dn, g_final):
    assert w_ada.shape[0] == 1, "single-layer block"
    d = D_MODEL
    groups = [(x_prompt, c_prompt), (x_sample, c_sample)]

    win = w_in[0].astype(BF16)
    ws = w_s[0].astype(BF16)
    bs = jnp.broadcast_to(b_s[0][:, :, None], (N_GROUPS_A, CHUNK, GA)).astype(F32)
    woa = w_out[0, :D_A].astype(BF16)
    wob = w_out[0, D_A:].astype(BF16)
    wr = jnp.pad(w_router[0], ((0, 0), (0, LANES - N_EXPERTS)))
    wrh = wr.astype(BF16)
    wrl = (wr - wrh.astype(F32)).astype(BF16)
    br = jnp.pad(b_router[0], (0, LANES - N_EXPERTS), constant_values=NEG).reshape(1, LANES)
    wgu = w_gu[0].astype(BF16)
    wdn = w_dn[0].astype(BF16)
    bgu = b_gu[0].reshape(N_EXPERTS, 1, 2 * D_FF)
    bdn = b_dn[0].reshape(N_EXPERTS, 1, d)
    bias = _attn_bias_tables(rpb[0])
    row = lambda a: a.reshape(1, -1)

    n_c = sum(c.shape[0] for _, c in groups)
    c_all = jnp.concatenate([c for _, c in groups] + [jnp.zeros((-n_c % 8, d), F32)], axis=0)
    mod_all = _ada(c_all, w_ada[0], row(b_ada[0])).reshape(-1, 6, d)

    per_group = []
    b0 = 0
    for x, c in groups:
        b, n, _ = x.shape
        assert n % TM_MIX == 0 and n % Q_TOK == 0 and (b * n) % ROUTE_CHUNK == 0
        x2 = x.reshape(b * n, d)
        mod6 = mod_all[b0:b0 + b]
        b0 += b
        an, q, k, v = _mix_in(x2, mod6, row(g_mix[0]), win, ws, bs, row(g_sgu[0]), row(g_out_a[0]), n)
        bn = _attn(q, k, v, bias, row(g_out_b[0]), b, n)
        x1, hf3, route, cnt = _mix_out(an, bn, x2, mod6, woa, wob, row(g_ffn[0]), wrh, wrl, br, n)
        per_group.append((x1, hf3, route, cnt, mod6, b, n))

    n_tiles = (ROUTE_CHUNK * TOP_K + N_EXPERTS * (TM_EXP - 1)) // TM_EXP + 1
    tile_start = jnp.arange(n_tiles, dtype=jnp.int32) * TM_EXP
    outs = []
    for x1, hf3, route, cnt, mod6, b, n in per_group:
        out = None
        for ci in range(b * n // ROUTE_CHUNK):
            tok0 = ci * ROUTE_CHUNK
            counts = cnt[ci, 0, :N_EXPERTS].astype(jnp.int32)
            padded = ((counts + TM_EXP - 1) // TM_EXP) * TM_EXP
            ends = jnp.cumsum(padded)
            starts = ends - padded
            n_used = (ends[-1] // TM_EXP).astype(jnp.int32).reshape(1)
            tile_expert = jnp.minimum(jnp.sum(tile_start[:, None] >= ends[None, :], axis=1), N_EXPERTS - 1)
            tile_expert = jnp.where(jnp.arange(n_tiles) < n_used[0], tile_expert,
                                    tile_expert[jnp.maximum(n_used[0] - 1, 0)]).astype(jnp.int32)
            tile_valid = jnp.clip((starts + counts)[tile_expert] - tile_start, 0, TM_EXP).astype(jnp.int32)
            rt = lax.slice_in_dim(route, tok0, tok0 + ROUTE_CHUNK, axis=0)
            idx = rt[:, :TOP_K].astype(jnp.int32)
            rank = rt[:, 2 * TOP_K:3 * TOP_K].astype(jnp.int32)
            pos = starts[idx] + rank
            posb = pos.reshape(ROUTE_CHUNK // SC_ROWS, SC_ROWS, TOP_K).transpose(0, 2, 1)

            xs3 = _dispatch(hf3, posb, tok0, ROUTE_CHUNK, n_tiles * TM_EXP)
            y3 = _experts(tile_expert, n_used, tile_valid, xs3, wgu, bgu, wdn, bdn)
            g = _collect(y3, posb, ROUTE_CHUNK)
            out = _combine(g, route, x1, mod6, row(g_final), out, tok0, n)
        outs.append(out.reshape(b, n, d))
    return tuple(outs)
```
